```python
import math
import jax, jax.numpy as jnp
from jax import lax
import numpy as np

D_MODEL = 2048
BATCH = 4
SEQ = 2048
DEPTH = 4
DEC_BATCH = 8
DEC_SEQ = 8
PAST_LEN = 16384
PAGE_SIZE = 128

D_A = D_MODEL // 4
DHA = 128
HA = D_A // DHA
MLSTM_CHUNK = 64
D_B = D_MODEL // 4
N_BLK_B = 8
BLK_B = D_B // N_BLK_B
CONV_W = 4
LRU_C = 8.0
D_C = D_MODEL // 2
DHC = 128
HC = D_C // DHC
N_IDX_HEADS = 8
D_IDX = 64
TOPK_MAX = 256
Q_BLOCK = 128
ROPE_THETA = 500000.0
ROT_C = DHC // 4
ROT_IDX = D_IDX // 4
D_MIX = D_A + D_B + D_C
D_FF = ((8 * D_MODEL + 3 * 256 - 1) // (3 * 256)) * 256
ALPHA = (2 * DEPTH) ** 0.25
BETA = (8 * DEPTH) ** -0.25
LN_EPS = 1e-5
SECTION_SIZES = (D_A, D_A, D_A, D_A, HA, HA, D_B, D_B, D_C, D_C, D_C, N_IDX_HEADS * D_IDX, D_IDX, N_IDX_HEADS)
N_IN = 4 * D_A + 2 * HA + 2 * D_B + 3 * D_C + N_IDX_HEADS * D_IDX + D_IDX + N_IDX_HEADS
F_GATE_OFFSET = 4 * D_A + HA

kernel_name = "hymba_mlstm_rglru_dsa_step"


def layer_norm(x, g, b):
    xf = x.astype(jnp.float32)
    mu = jnp.mean(xf, axis=-1, keepdims=True)
    var = jnp.mean(jnp.square(xf - mu), axis=-1, keepdims=True)
    return ((xf - mu) * lax.rsqrt(var + LN_EPS)).astype(x.dtype) * g + b


def head_norm(h, g):
    hf = h.astype(jnp.float32)
    mu = jnp.mean(hf, axis=-1, keepdims=True)
    var = jnp.mean(jnp.square(hf - mu), axis=-1, keepdims=True)
    return ((hf - mu) * lax.rsqrt(var + LN_EPS)).astype(h.dtype) * g


def rotary(x, pos, rot_dim):
    half = rot_dim // 2
    inv = jnp.power(ROPE_THETA, -jnp.arange(half, dtype=jnp.float32) / half)
    ang = pos.astype(jnp.float32)[:, None] * inv[None, :]
    cos = jnp.cos(ang)[:, None, :].astype(x.dtype)
    sin = jnp.sin(ang)[:, None, :].astype(x.dtype)
    x1, x2, rest = x[..., :half], x[..., half:rot_dim], x[..., rot_dim:]
    return jnp.concatenate([x1 * cos - x2 * sin, x2 * cos + x1 * sin, rest], axis=-1)


def mlstm_chunkwise(q, k, v, log_i, log_f, c0, n0, m0, chunk):
    B, T, H, D = q.shape
    nc = T // chunk
    f32 = jnp.float32

    def to_chunks(a):
        a = a.reshape((B, nc, chunk) + a.shape[2:])
        return jnp.moveaxis(jnp.moveaxis(a, 3, 2), 1, 0)

    xs = (to_chunks(q.astype(f32) * D ** -0.5), to_chunks(k.astype(f32)), to_chunks(v.astype(f32)),
          to_chunks(log_i.astype(f32)), to_chunks(log_f.astype(f32)))
    causal = jnp.tril(jnp.ones((chunk, chunk), dtype=bool))

    def step(carry, inp):
        c, n, m = carry
        qc, kc, vc, li, lf = inp
        b = jnp.cumsum(lf, axis=-1)
        dmat = jnp.where(causal, b[..., :, None] - b[..., None, :] + li[..., None, :], -jnp.inf)
        inter = b + m[..., None]
        m_t = jnp.maximum(inter, jnp.max(dmat, axis=-1))
        w_intra = jnp.exp(dmat - m_t[..., None])
        w_inter = jnp.exp(inter - m_t)
        s = jnp.einsum('bhtd,bhsd->bhts', qc, kc) * w_intra
        num = jnp.einsum('bhts,bhse->bhte', s, vc) + w_inter[..., None] * jnp.einsum('bhtd,bhde->bhte', qc, c)
        den = jnp.sum(s, axis=-1) + w_inter * jnp.einsum('bhtd,bhd->bht', qc, n)
        h = num / jnp.maximum(jnp.abs(den), jnp.exp(-m_t))[..., None]
        b_last = b[..., -1]
        g = b_last[..., None] - b + li
        m_new = jnp.maximum(b_last + m, jnp.max(g, axis=-1))
        wg = jnp.exp(g - m_new[..., None])
        wc = jnp.exp(b_last + m - m_new)
        c_new = wc[..., None, None] * c + jnp.einsum('bhs,bhsd,bhse->bhde', wg, kc, vc)
        n_new = wc[..., None] * n + jnp.einsum('bhs,bhsd->bhd', wg, kc)
        return (c_new, n_new, m_new), h

    (c1, n1, m1), h = lax.scan(step, (c0.astype(f32), n0.astype(f32), m0.astype(f32)), xs)
    h = jnp.moveaxis(jnp.moveaxis(h, 0, 1), 2, 3).reshape(B, T, H, D)
    return h.astype(q.dtype), (c1.astype(c0.dtype), n1.astype(n0.dtype), m1.astype(m0.dtype))


def block_diag(x, w, b):
    xb = x.reshape(x.shape[:-1] + (N_BLK_B, BLK_B))
    return jnp.einsum('btnd,nde->btne', xb, w).reshape(x.shape) + b


def rglru_block(xb, gb, conv_buf, h0, w_conv, b_conv, w_a, b_a, w_x, b_x, lam):
    T = xb.shape[1]
    xpad = jnp.concatenate([conv_buf.astype(xb.dtype), xb], axis=1)
    xc = b_conv + sum(xpad[:, j:j + T] * w_conv[j] for j in range(CONV_W))
    new_buf = xpad[:, T:]
    f32 = jnp.float32
    r = jax.nn.sigmoid(block_diag(xc, w_a, b_a).astype(f32))
    i = jax.nn.sigmoid(block_diag(xc, w_x, b_x).astype(f32))
    log_a = -LRU_C * r * jax.nn.softplus(-lam.astype(f32))
    a = jnp.exp(log_a)
    u = jnp.sqrt(-jnp.expm1(2.0 * log_a)) * (i * xc.astype(f32))
    u = u.at[:, 0].add(a[:, 0] * h0.astype(f32))

    def comb(left, right):
        a1, b1 = left
        a2, b2 = right
        return a1 * a2, a2 * b1 + b2

    _, h = lax.associative_scan(comb, (a, u), axis=1)
    y = h.astype(xb.dtype) * jax.nn.gelu(gb)
    return y, new_buf, h[:, -1].astype(h0.dtype)


def indexer_scores(qi, ki, wi):
    dots = jax.nn.relu(jnp.einsum('bthd,bsd->bths', qi, ki))
    return jnp.einsum('bths,bth->bts', dots, wi).astype(jnp.float32)


def gather_rows(table, idx):
    return jax.vmap(lambda t, i: t[i])(table, idx)


def attend_selected(q, ks, vs, valid):
    logits = jnp.einsum('bthd,btkhd->bthk', q, ks).astype(jnp.float32) * DHC ** -0.5
    logits = jnp.where(valid[:, :, None, :], logits, -jnp.inf)
    p = jax.nn.softmax(logits, axis=-1).astype(vs.dtype)
    return jnp.einsum('bthk,btkhd->bthd', p, vs)


def dsa_prompt(q, k, v, qi, ki, wi):
    B, S = q.shape[:2]
    topk = min(TOPK_MAX, S // 4)
    nb = S // Q_BLOCK

    def blocks(a):
        return jnp.moveaxis(a.reshape((B, nb, Q_BLOCK) + a.shape[2:]), 1, 0)

    key_pos = jnp.arange(S)

    def one_block(args):
        qb, qib, wib, start = args
        qpos = start + jnp.arange(Q_BLOCK)
        sc = indexer_scores(qib, ki, wib)
        sc = jnp.where((key_pos[None, :] <= qpos[:, None])[None], sc, -jnp.inf)
        _, idx = lax.top_k(sc, topk)
        valid = idx <= qpos[None, :, None]
        return attend_selected(qb, gather_rows(k, idx), gather_rows(v, idx), valid)

    out = lax.map(one_block, (blocks(q), blocks(qi), blocks(wi), jnp.arange(nb) * Q_BLOCK))
    return jnp.moveaxis(out, 0, 1).reshape(q.shape)


def dsa_sample(q, k_new, v_new, qi, ki_new, wi, cache_k, cache_v, cache_ik, page_table):
    Bd, T = q.shape[:2]
    past = page_table.shape[1] * PAGE_SIZE
    L = past + T
    topk = min(TOPK_MAX, L // 4)
    ki_past = cache_ik[page_table].reshape(Bd, past, D_IDX)
    ki_all = jnp.concatenate([ki_past, ki_new.astype(ki_past.dtype)], axis=1)
    sc = indexer_scores(qi, ki_all, wi)
    qpos = past + jnp.arange(T)
    sc = jnp.where((jnp.arange(L)[None, :] <= qpos[:, None])[None], sc, -jnp.inf)
    _, idx = lax.top_k(sc, topk)
    valid = idx <= qpos[None, :, None]
    in_past = idx < past
    pidx = jnp.minimum(idx, past - 1)
    phys = jax.vmap(lambda pt, i: pt[i])(page_table, pidx // PAGE_SIZE)
    off = pidx % PAGE_SIZE
    nidx = jnp.clip(idx - past, 0, T - 1)
    sel = in_past[..., None, None]
    ks = jnp.where(sel, cache_k[phys, off], gather_rows(k_new, nidx))
    vs = jnp.where(sel, cache_v[phys, off], gather_rows(v_new, nidx))
    return attend_selected(q, ks, vs, valid)


def trunk_layer(x, pos, p, mstate, conv_buf, h0, attend):
    (w_in, b_in, g_mlstm, w_conv, b_conv, w_a, b_a, w_x, b_x, lam,
     w_out, ln1_g, ln1_b, w_gu, w_down, ln2_g, ln2_b) = p
    B, T, _ = x.shape
    proj = jnp.einsum('btd,dn->btn', x, w_in) + b_in
    pts = np.cumsum(SECTION_SIZES)[:-1].tolist()
    (qa, ka, va, oa, ia, fa, xb, gb, qc, kc, vc, qi, ki, wi) = jnp.split(proj, pts, axis=-1)

    def heads(a, h):
        return a.reshape(B, T, h, -1)

    c0, n0, m0 = mstate
    chunk = math.gcd(T, MLSTM_CHUNK)
    h_a, new_m = mlstm_chunkwise(heads(qa, HA), heads(ka, HA), heads(va, HA), ia,
                                 jax.nn.log_sigmoid(fa.astype(jnp.float32)), c0, n0, m0, chunk)
    h_a = head_norm(h_a, g_mlstm.reshape(HA, DHA)) * jax.nn.sigmoid(heads(oa, HA))
    y_b, new_buf, h1 = rglru_block(xb, gb, conv_buf, h0, w_conv, b_conv, w_a, b_a, w_x, b_x, lam)
    qc = rotary(heads(qc, HC), pos, ROT_C)
    kc = rotary(heads(kc, HC), pos, ROT_C)
    vc = heads(vc, HC)
    qi = rotary(heads(qi, N_IDX_HEADS), pos, ROT_IDX) * D_IDX ** -0.5
    ki = rotary(ki[:, :, None, :], pos, ROT_IDX)[:, :, 0]
    wi = wi * N_IDX_HEADS ** -0.5
    y_c = attend(qc, kc, vc, qi, ki, wi)
    mix = jnp.concatenate([h_a.reshape(B, T, D_A), y_b, y_c.reshape(B, T, D_C)], axis=-1)
    x = layer_norm(ALPHA * x + jnp.einsum('btm,md->btd', mix, w_out), ln1_g, ln1_b)
    g, u = jnp.split(jnp.einsum('btd,df->btf', x, w_gu), 2, axis=-1)
    x = layer_norm(ALPHA * x + jnp.einsum('btf,fd->btd', jax.nn.silu(g) * u, w_down), ln2_g, ln2_b)
    return x, (kc, vc, ki), new_m, new_buf, h1


def setup_inputs(seed: int = 0) -> dict:
    key = jax.random.key(seed)
    ks = iter(jax.random.split(key, 40))

    def nrm(shape, scale):
        return jax.random.normal(next(ks), shape, jnp.float32) * scale

    n_pages = PAST_LEN // PAGE_SIZE
    n_used = DEC_BATCH * n_pages
    n_pool = n_used + max(1, n_used // 4)
    inp = {}
    inp['x_prompt'] = nrm((BATCH, SEQ, D_MODEL), 1.0)
    inp['x_sample'] = nrm((DEC_BATCH, DEC_SEQ, D_MODEL), 1.0)
    inp['cache_k'] = nrm((DEPTH, n_pool, PAGE_SIZE, HC, DHC), 1.0)
    inp['cache_v'] = nrm((DEPTH, n_pool, PAGE_SIZE, HC, DHC), 1.0)
    inp['cache_idx_k'] = nrm((DEPTH, n_pool, PAGE_SIZE, D_IDX), 1.0)
    inp['page_table'] = jax.random.permutation(next(ks), n_pool)[:n_used].reshape(DEC_BATCH, n_pages).astype(jnp.int32)
    inp['state_mlstm_c'] = nrm((DEPTH, DEC_BATCH, HA, DHA, DHA), 0.1)
    inp['state_mlstm_n'] = nrm((DEPTH, DEC_BATCH, HA, DHA), 0.1)
    inp['state_mlstm_m'] = nrm((DEPTH, DEC_BATCH, HA), 1.0)
    inp['state_conv'] = nrm((DEPTH, DEC_BATCH, CONV_W - 1, D_B), 1.0)
    inp['state_rglru_h'] = nrm((DEPTH, DEC_BATCH, D_B), 0.5)
    inp['w_in'] = nrm((DEPTH, D_MODEL, N_IN), D_MODEL ** -0.5)
    inp['b_in'] = nrm((DEPTH, N_IN), 0.02).at[:, F_GATE_OFFSET:F_GATE_OFFSET + HA].add(
        jnp.linspace(3.0, 6.0, HA, dtype=jnp.float32))
    inp['g_mlstm'] = 1.0 + nrm((DEPTH, D_A), 0.02)
    inp['w_conv'] = nrm((DEPTH, CONV_W, D_B), CONV_W ** -0.5)
    inp['b_conv'] = nrm((DEPTH, D_B), 0.02)
    inp['w_a'] = nrm((DEPTH, N_BLK_B, BLK_B, BLK_B), BLK_B ** -0.5)
    inp['b_a'] = nrm((DEPTH, D_B), 0.02)
    inp['w_x'] = nrm((DEPTH, N_BLK_B, BLK_B, BLK_B), BLK_B ** -0.5)
    inp['b_x'] = nrm((DEPTH, D_B), 0.02)
    u = jax.random.uniform(next(ks), (DEPTH, D_B), jnp.float32, 0.9, 0.999) ** (1.0 / LRU_C)
    inp['lam'] = jnp.log(u) - jnp.log1p(-u)
    inp['w_out'] = nrm((DEPTH, D_MIX, D_MODEL), D_MIX ** -0.5 * BETA)
    inp['ln1_g'] = 1.0 + nrm((DEPTH, D_MODEL), 0.02)
    inp['ln1_b'] = nrm((DEPTH, D_MODEL), 0.02)
    inp['w_gu'] = nrm((DEPTH, D_MODEL, 2 * D_FF), D_MODEL ** -0.5)
    inp['w_down'] = nrm((DEPTH, D_FF, D_MODEL), D_FF ** -0.5 * BETA)
    inp['ln2_g'] = 1.0 + nrm((DEPTH, D_MODEL), 0.02)
    inp['ln2_b'] = nrm((DEPTH, D_MODEL), 0.02)
    return inp


def reference(x_prompt, x_sample, cache_k, cache_v, cache_idx_k, page_table, state_mlstm_c, state_mlstm_n,
              state_mlstm_m, state_conv, state_rglru_h, w_in, b_in, g_mlstm, w_conv, b_conv, w_a, b_a, w_x, b_x,
              lam, w_out, ln1_g, ln1_b, w_gu, w_down, ln2_g, ln2_b):
    B, T_p = x_prompt.shape[:2]
    T_s = x_sample.shape[1]
    pos_p = jnp.arange(T_p)
    pos_s = page_table.shape[1] * PAGE_SIZE + jnp.arange(T_s)
    sdt = state_mlstm_c.dtype
    yp, ys = x_prompt, x_sample
    acc = [[] for _ in range(16)]
    for l in range(DEPTH):
        p = (w_in[l], b_in[l], g_mlstm[l], w_conv[l], b_conv[l], w_a[l], b_a[l], w_x[l], b_x[l], lam[l],
             w_out[l], ln1_g[l], ln1_b[l], w_gu[l], w_down[l], ln2_g[l], ln2_b[l])
        zero_m = (jnp.zeros((B, HA, DHA, DHA), sdt), jnp.zeros((B, HA, DHA), sdt), jnp.zeros((B, HA), sdt))
        yp, kv_p, m_p, buf_p, h_p = trunk_layer(
            yp, pos_p, p, zero_m, jnp.zeros((B, CONV_W - 1, D_B), x_prompt.dtype),
            jnp.zeros((B, D_B), state_rglru_h.dtype), dsa_prompt)

        def attend_s(q, k, v, qi, ki, wi, l=l):
            return dsa_sample(q, k, v, qi, ki, wi, cache_k[l], cache_v[l], cache_idx_k[l], page_table)

        ys, kv_s, m_s, buf_s, h_s = trunk_layer(
            ys, pos_s, p, (state_mlstm_c[l], state_mlstm_n[l], state_mlstm_m[l]), state_conv[l],
            state_rglru_h[l], attend_s)
        for j, a in enumerate(kv_p + kv_s + m_p + (buf_p, h_p) + m_s + (buf_s, h_s)):
            acc[j].append(a)
    (k_p, v_p, ik_p, k_s, v_s, ik_s, c_p, n_p, mm_p, conv_p, hh_p,
     c_s, n_s, mm_s, conv_s, hh_s) = [jnp.stack(a) for a in acc]
    return (yp, ys, k_p, v_p, ik_p, k_s, v_s, ik_s, c_p, n_p, mm_p, conv_p, hh_p, c_s, n_s, mm_s, conv_s, hh_s)
```

```python
import functools

import numpy as np
import jax
import jax.numpy as jnp
from jax import lax
from jax.experimental import pallas as pl
from jax.experimental.pallas import tpu as pltpu

D_MODEL = 2048
DEPTH = 4
PAGE_SIZE = 128
DHA = 128
HA = 4
D_A = HA * DHA
D_B = 512
N_BLK_B = 8
BLK_B = D_B // N_BLK_B
CONV_W = 4
LRU_C = 8.0
DHC = 128
HC = 8
D_C = HC * DHC
N_IDX_HEADS = 8
D_IDX = 64
TOPK_MAX = 256
ROPE_THETA = 500000.0
ROT_C = DHC // 4
ROT_IDX = D_IDX // 4
D_FF = 5632
ALPHA = (2 * DEPTH) ** 0.25
LN_EPS = 1e-5

LANES = 128
VMEM_LIMIT = 56 * 1024 * 1024
MXU_DTYPE = jnp.bfloat16
NEG_INF = float("-inf")
INT_MIN = -2147483648
NEG_INF_KEY = -2139095041

OFF_A = 0
OFF_B = 4 * D_A
OFF_C = OFF_B + 2 * D_B
OFF_QI = OFF_C + 3 * D_C
OFF_TAIL = OFF_QI + N_IDX_HEADS * D_IDX
TAIL_WI = D_IDX
TAIL_IA = D_IDX + N_IDX_HEADS
TAIL_FA = TAIL_IA + HA
N_PROJ = OFF_TAIL + 2 * LANES


def _cparams(*sem):
    return pltpu.CompilerParams(dimension_semantics=sem, vmem_limit_bytes=VMEM_LIMIT)


def _dot(a, b):
    return jnp.dot(a, b, preferred_element_type=jnp.float32)


def _dot_nt(a, b):
    return lax.dot_general(a, b, (((1,), (1,)), ((), ())), preferred_element_type=jnp.float32)


def _dot_tn(a, b):
    return lax.dot_general(a, b, (((0,), (0,)), ((), ())), preferred_element_type=jnp.float32)


def _layer_norm_rows(y, g, b):
    mu = jnp.mean(y, axis=-1, keepdims=True)
    d = y - mu
    var = jnp.mean(d * d, axis=-1, keepdims=True)
    return d * lax.rsqrt(var + LN_EPS) * g + b


def _proj_kernel(x_ref, w_ref, b_ref, o_ref):
    o_ref[...] = _dot(x_ref[...], w_ref[...]) + b_ref[...]


def _proj(x, w, b, tm, tn):
    m, k = x.shape
    n = w.shape[1]
    return pl.pallas_call(
        _proj_kernel,
        grid=(m // tm, n // tn),
        in_specs=[pl.BlockSpec((tm, k), lambda i, j: (i, 0)),
                  pl.BlockSpec((k, tn), lambda i, j: (0, j)),
                  pl.BlockSpec((1, tn), lambda i, j: (0, j))],
        out_specs=pl.BlockSpec((tm, tn), lambda i, j: (i, j)),
        out_shape=jax.ShapeDtypeStruct((m, n), jnp.float32),
        compiler_params=_cparams("parallel", "arbitrary"),
        name="proj",
    )(x, w, b)


def _outproj_kernel(a_ref, b_ref, c_ref, w_ref, x_ref, g_ref, bb_ref, of_ref, ob_ref):
    acc = _dot(a_ref[...], w_ref[0:D_A, :])
    acc = acc + _dot(b_ref[...], w_ref[D_A:D_A + D_B, :])
    acc = acc + _dot(c_ref[...], w_ref[D_A + D_B:, :])
    y = _layer_norm_rows(ALPHA * x_ref[...] + acc, g_ref[...], bb_ref[...])
    of_ref[...] = y
    ob_ref[...] = y.astype(ob_ref.dtype)


def _outproj(mix_a, mix_b, mix_c, w, x, g, b, tm):
    m = x.shape[0]
    row = lambda i: (i, 0)
    fix = lambda i: (0, 0)
    return pl.pallas_call(
        _outproj_kernel,
        grid=(m // tm,),
        in_specs=[pl.BlockSpec((tm, D_A), row), pl.BlockSpec((tm, D_B), row), pl.BlockSpec((tm, D_C), row),
                  pl.BlockSpec((D_A + D_B + D_C, D_MODEL), fix),
                  pl.BlockSpec((tm, D_MODEL), row),
                  pl.BlockSpec((1, D_MODEL), fix), pl.BlockSpec((1, D_MODEL), fix)],
        out_specs=[pl.BlockSpec((tm, D_MODEL), row), pl.BlockSpec((tm, D_MODEL), row)],
        out_shape=[jax.ShapeDtypeStruct((m, D_MODEL), jnp.float32),
                   jax.ShapeDtypeStruct((m, D_MODEL), MXU_DTYPE)],
        compiler_params=_cparams("parallel"),
        name="outproj",
    )(mix_a, mix_b, mix_c, w, x, g, b)


def _ffn_kernel(xb_ref, wg_ref, wu_ref, wd_ref, xf_ref, g_ref, b_ref, of_ref, ob_ref, acc_ref):
    f = pl.program_id(1)

    @pl.when(f == 0)
    def _():
        acc_ref[...] = jnp.zeros_like(acc_ref)

    x = xb_ref[...]
    gate = _dot(x, wg_ref[...])
    up = _dot(x, wu_ref[...])
    h = (jax.nn.silu(gate) * up).astype(wd_ref.dtype)
    acc_ref[...] += _dot(h, wd_ref[...])

    @pl.when(f == pl.num_programs(1) - 1)
    def _():
        y = _layer_norm_rows(ALPHA * xf_ref[...] + acc_ref[...], g_ref[...], b_ref[...])
        of_ref[...] = y
        ob_ref[...] = y.astype(ob_ref.dtype)


def _ffn(xb, xf, w_gu, w_down, g, b, tm, tf):
    m = xb.shape[0]
    nf = D_FF // tf
    row = lambda i, f: (i, 0)
    fix = lambda i, f: (0, 0)
    return pl.pallas_call(
        _ffn_kernel,
        grid=(m // tm, nf),
        in_specs=[pl.BlockSpec((tm, D_MODEL), row),
                  pl.BlockSpec((D_MODEL, tf), lambda i, f: (0, f)),
                  pl.BlockSpec((D_MODEL, tf), lambda i, f: (0, nf + f)),
                  pl.BlockSpec((tf, D_MODEL), lambda i, f: (f, 0)),
                  pl.BlockSpec((tm, D_MODEL), row),
                  pl.BlockSpec((1, D_MODEL), fix), pl.BlockSpec((1, D_MODEL), fix)],
        out_specs=[pl.BlockSpec((tm, D_MODEL), row), pl.BlockSpec((tm, D_MODEL), row)],
        out_shape=[jax.ShapeDtypeStruct((m, D_MODEL), jnp.float32),
                   jax.ShapeDtypeStruct((m, D_MODEL), MXU_DTYPE)],
        scratch_shapes=[pltpu.VMEM((tm, D_MODEL), jnp.float32)],
        compiler_params=_cparams("parallel", "arbitrary"),
        name="ffn",
    )(xb, w_gu, w_gu, w_down, xf, g, b)


def _rotate(x, cos, sin_hi, sin_lo, half):
    n = x.shape[-1]
    return x * cos + pltpu.roll(x, half, 1) * sin_hi + pltpu.roll(x, n - half, 1) * sin_lo


def _rope_kernel(q_ref, k_ref, v_ref, qi_ref, tail_ref, cc_ref, csh_ref, csl_ref, ic_ref, ish_ref, isl_ref,
                 kf_ref, vf_ref, kif_ref, qb_ref, kb_ref, vb_ref, qib_ref, kib_ref):
    cc, csh, csl = cc_ref[...], csh_ref[...], csl_ref[...]
    ic, ish, isl = ic_ref[...], ish_ref[...], isl_ref[...]
    for h in range(HC):
        sl = slice(h * DHC, (h + 1) * DHC)
        qr = _rotate(q_ref[:, sl], cc, csh, csl, ROT_C // 2)
        kr = _rotate(k_ref[:, sl], cc, csh, csl, ROT_C // 2)
        qb_ref[:, sl] = qr.astype(qb_ref.dtype)
        kf_ref[:, sl] = kr
        kb_ref[:, sl] = kr.astype(kb_ref.dtype)
    v = v_ref[...]
    vf_ref[...] = v
    vb_ref[...] = v.astype(vb_ref.dtype)
    for g in range(N_IDX_HEADS * D_IDX // LANES):
        sl = slice(g * LANES, (g + 1) * LANES)
        qir = _rotate(qi_ref[:, sl], ic, ish, isl, ROT_IDX // 2) * (D_IDX ** -0.5)
        qib_ref[:, sl] = qir.astype(qib_ref.dtype)
    kir = _rotate(tail_ref[...], ic, ish, isl, ROT_IDX // 2)[:, 0:D_IDX]
    kif_ref[...] = kir
    kib_ref[...] = kir.astype(kib_ref.dtype)


def _rope(proj, tabs, tm):
    m = proj.shape[0]
    row = lambda i: (i, 0)
    cb = lambda width, off: pl.BlockSpec((tm, width), lambda i: (i, off // width))
    f32, bf = jnp.float32, MXU_DTYPE
    return pl.pallas_call(
        _rope_kernel,
        grid=(m // tm,),
        in_specs=[cb(D_C, OFF_C), cb(D_C, OFF_C + D_C), cb(D_C, OFF_C + 2 * D_C),
                  cb(N_IDX_HEADS * D_IDX, OFF_QI), cb(LANES, OFF_TAIL)]
                 + [pl.BlockSpec((tm, LANES), row)] * 6,
        out_specs=[pl.BlockSpec((tm, D_C), row), pl.BlockSpec((tm, D_C), row), pl.BlockSpec((tm, D_IDX), row),
                   pl.BlockSpec((tm, D_C), row), pl.BlockSpec((tm, D_C), row), pl.BlockSpec((tm, D_C), row),
                   pl.BlockSpec((tm, N_IDX_HEADS * D_IDX), row), pl.BlockSpec((tm, D_IDX), row)],
        out_shape=[jax.ShapeDtypeStruct((m, D_C), f32), jax.ShapeDtypeStruct((m, D_C), f32),
                   jax.ShapeDtypeStruct((m, D_IDX), f32),
                   jax.ShapeDtypeStruct((m, D_C), bf), jax.ShapeDtypeStruct((m, D_C), bf),
                   jax.ShapeDtypeStruct((m, D_C), bf),
                   jax.ShapeDtypeStruct((m, N_IDX_HEADS * D_IDX), bf), jax.ShapeDtypeStruct((m, D_IDX), bf)],
        compiler_params=_cparams("parallel"),
        name="rope",
    )(proj, proj, proj, proj, proj, *tabs)


def _rope_tables(pos, reps):
    def tables(rot_dim, period):
        half = rot_dim // 2
        inv = jnp.power(ROPE_THETA, -jnp.arange(half, dtype=jnp.float32) / half)
        ang = pos.astype(jnp.float32)[:, None] * inv[None, :]
        cos, sin = jnp.cos(ang), jnp.sin(ang)
        t = pos.shape[0]
        pad = jnp.zeros((t, period - rot_dim), jnp.float32)
        zero = jnp.zeros((t, half), jnp.float32)
        c = jnp.concatenate([cos, cos, pad + 1.0], axis=1)
        s_hi = jnp.concatenate([zero, sin, pad], axis=1)
        s_lo = jnp.concatenate([-sin, zero, pad], axis=1)
        return [jnp.tile(a, (reps, LANES // period)) for a in (c, s_hi, s_lo)]
    return tables(ROT_C, DHC) + tables(ROT_IDX, D_IDX)


def _mlstm_kernel(q_ref, k_ref, v_ref, o_ref, gcol_ref, grow_ref, c0_ref, n0_ref, m0_ref, gn_ref,
                  h_ref, c_ref, n_ref, m_ref, *, chunk):
    L = chunk
    tt = q_ref.shape[0]

    @pl.when(pl.program_id(1) == 0)
    def _():
        c_ref[...] = c0_ref[...]
        n_ref[...] = n0_ref[...]
        m_ref[...] = m0_ref[...]

    r_i = lax.broadcasted_iota(jnp.int32, (L, L), 0)
    c_i = lax.broadcasted_iota(jnp.int32, (L, L), 1)
    causal = r_i >= c_i

    def chunk_body(ci, carry):
        rows = pl.ds(pl.multiple_of(ci * L, L), L)
        grow = grow_ref[ci]
        for h in range(HA):
            sl = slice(h * DHA, (h + 1) * DHA)
            q = q_ref[rows, sl] * (DHA ** -0.5)
            k = k_ref[rows, sl]
            v = v_ref[rows, sl]
            li_col = gcol_ref[rows, h:h + 1]
            lf_col = jax.nn.log_sigmoid(gcol_ref[rows, HA + h:HA + h + 1])
            li_row = grow[h:h + 1, :]
            lf_row = jax.nn.log_sigmoid(grow[HA + h:HA + h + 1, :])
            b_row = jnp.sum(jnp.where(r_i <= c_i, lf_col, 0.0), axis=0, keepdims=True)
            b_col = jnp.sum(jnp.where(causal, lf_row, 0.0), axis=1, keepdims=True)
            m_old = m_ref[h:h + 1, 0:1]
            dmat = jnp.where(causal, b_col - b_row + li_row, NEG_INF)
            inter = b_col + m_old
            m_t = jnp.maximum(inter, jnp.max(dmat, axis=1, keepdims=True))
            w_intra = jnp.exp(dmat - m_t)
            w_inter = jnp.exp(inter - m_t)
            qm = q.astype(MXU_DTYPE)
            km = k.astype(MXU_DTYPE)
            vm = v.astype(MXU_DTYPE)
            c_old = c_ref[h]
            n_old = n_ref[h:h + 1, :]
            s = _dot_nt(qm, km) * w_intra
            num = _dot(s.astype(MXU_DTYPE), vm) + w_inter * _dot(qm, c_old.astype(MXU_DTYPE))
            den = jnp.sum(s, axis=1, keepdims=True) + w_inter * jnp.sum(q * n_old, axis=1, keepdims=True)
            hh = num / jnp.maximum(jnp.abs(den), jnp.exp(-m_t))
            b_last = b_row[:, L - 1:L]
            g_row = b_last - b_row + li_row
            m_new = jnp.maximum(b_last + m_old, jnp.max(g_row, axis=1, keepdims=True))
            wg_col = jnp.exp(b_last - b_col + li_col - m_new)
            wc = jnp.exp(b_last + m_old - m_new)
            kw = k * wg_col
            c_ref[h] = wc * c_old + _dot_tn(kw.astype(MXU_DTYPE), vm)
            n_ref[h:h + 1, :] = wc * n_old + jnp.sum(kw, axis=0, keepdims=True)
            m_ref[h:h + 1, :] = jnp.broadcast_to(m_new, (1, DHA))
            mu = jnp.mean(hh, axis=1, keepdims=True)
            d = hh - mu
            var = jnp.mean(d * d, axis=1, keepdims=True)
            hn = d * lax.rsqrt(var + LN_EPS) * gn_ref[h:h + 1, :]
            h_ref[rows, sl] = (hn * jax.nn.sigmoid(o_ref[rows, sl])).astype(h_ref.dtype)
        return carry

    lax.fori_loop(0, tt // L, chunk_body, 0)


def _mlstm(proj3, gcol, grow, c0, n0, m0b, gn, chunk, tt):
    bsz, t, _ = proj3.shape
    sec = lambda j: pl.BlockSpec((None, tt, D_A), lambda b, i: (b, i, j))
    st4 = pl.BlockSpec((None, HA, DHA, DHA), lambda b, i: (b, 0, 0, 0))
    st3 = pl.BlockSpec((None, HA, DHA), lambda b, i: (b, 0, 0))
    return pl.pallas_call(
        functools.partial(_mlstm_kernel, chunk=chunk),
        grid=(bsz, t // tt),
        in_specs=[sec(0), sec(1), sec(2), sec(3),
                  pl.BlockSpec((None, tt, 2 * HA), lambda b, i: (b, i, 0)),
                  pl.BlockSpec((None, tt // chunk, 2 * HA, chunk), lambda b, i: (b, i, 0, 0)),
                  st4, st3, st3,
                  pl.BlockSpec((HA, DHA), lambda b, i: (0, 0))],
        out_specs=[pl.BlockSpec((None, tt, D_A), lambda b, i: (b, i, 0)), st4, st3, st3],
        out_shape=[jax.ShapeDtypeStruct((bsz, t, D_A), MXU_DTYPE),
                   jax.ShapeDtypeStruct((bsz, HA, DHA, DHA), jnp.float32),
                   jax.ShapeDtypeStruct((bsz, HA, DHA), jnp.float32),
                   jax.ShapeDtypeStruct((bsz, HA, DHA), jnp.float32)],
        compiler_params=_cparams("parallel", "arbitrary"),
        name="mlstm",
    )(proj3, proj3, proj3, proj3, gcol, grow, c0, n0, m0b, gn)


SUBLANES = 8


def _rglru_kernel(x_ref, g_ref, cs_ref, h0_ref, wc_ref, bc_ref, wa_ref, ba_ref, wx_ref, bx_ref, lam_ref,
                  y_ref, ncs_ref, hl_ref, xpad, a_s, u_s, *, rows_per_step):
    t = x_ref.shape[0]
    R = rows_per_step
    pad = SUBLANES
    xpad[0:pad, :] = jnp.zeros((pad, LANES), jnp.float32)
    xpad[pad - (CONV_W - 1):pad, :] = cs_ref[...]
    xpad[pad:pad + t, :] = x_ref[...]
    ncs_ref[...] = xpad[pad + t - (CONV_W - 1):pad + t, :]
    sub = lax.broadcasted_iota(jnp.int32, (R, LANES), 0) % SUBLANES
    for r in range(t // R):
        base = pad + r * R - (CONV_W - 1)
        s = xpad[base:base + R, :] * wc_ref[0:1, :]
        for j in range(1, CONV_W):
            s = s + xpad[base + j:base + j + R, :] * wc_ref[j:j + 1, :]
        xc = bc_ref[...] + s
        xm = xc.astype(MXU_DTYPE)
        rg = jax.nn.sigmoid(_dot(xm, wa_ref[...]) + ba_ref[...])
        ig = jax.nn.sigmoid(_dot(xm, wx_ref[...]) + bx_ref[...])
        log_a = lam_ref[...] * rg
        a = jnp.exp(log_a)
        th = jnp.tanh(log_a)
        u = jnp.sqrt(-2.0 * th / (1.0 - th)) * (ig * xc)
        for d in (1, 2, 4):
            keep = sub >= d
            u = jnp.where(keep, u + a * pltpu.roll(u, d, 0), u)
            a = jnp.where(keep, a * pltpu.roll(a, d, 0), a)
        a_s[r * R:(r + 1) * R, :] = a
        u_s[r * R:(r + 1) * R, :] = u

    def group(gi, h_prev):
        rows = pl.ds(pl.multiple_of(gi * SUBLANES, SUBLANES), SUBLANES)
        h8 = u_s[rows, :] + a_s[rows, :] * h_prev
        u_s[rows, :] = h8
        return h8[SUBLANES - 1:SUBLANES, :]

    hl_ref[...] = lax.fori_loop(0, t // SUBLANES, group, h0_ref[...])
    for r in range(t // R):
        rs = slice(r * R, (r + 1) * R)
        y_ref[rs, :] = (u_s[rs, :] * jax.nn.gelu(g_ref[rs, :])).astype(y_ref.dtype)


def _rglru(proj3, conv_state, h0, w_conv, b_conv, wa_bd, b_a, wx_bd, b_x, lam_c, rows_per_step):
    bsz, t, _ = proj3.shape
    nct = D_B // LANES
    vec = pl.BlockSpec((1, LANES), lambda b, c: (0, c))
    bd = pl.BlockSpec((None, LANES, LANES), lambda b, c: (c, 0, 0))
    return pl.pallas_call(
        functools.partial(_rglru_kernel, rows_per_step=rows_per_step),
        grid=(bsz, nct),
        in_specs=[pl.BlockSpec((None, t, LANES), lambda b, c: (b, 0, OFF_B // LANES + c)),
                  pl.BlockSpec((None, t, LANES), lambda b, c: (b, 0, (OFF_B + D_B) // LANES + c)),
                  pl.BlockSpec((None, CONV_W - 1, LANES), lambda b, c: (b, 0, c)),
                  pl.BlockSpec((None, 1, LANES), lambda b, c: (b, 0, c)),
                  pl.BlockSpec((CONV_W, LANES), lambda b, c: (0, c)),
                  vec, bd, vec, bd, vec, vec],
        out_specs=[pl.BlockSpec((None, t, LANES), lambda b, c: (b, 0, c)),
                   pl.BlockSpec((None, CONV_W - 1, LANES), lambda b, c: (b, 0, c)),
                   pl.BlockSpec((None, 1, LANES), lambda b, c: (b, 0, c))],
        out_shape=[jax.ShapeDtypeStruct((bsz, t, D_B), MXU_DTYPE),
                   jax.ShapeDtypeStruct((bsz, CONV_W - 1, D_B), jnp.float32),
                   jax.ShapeDtypeStruct((bsz, 1, D_B), jnp.float32)],
        scratch_shapes=[pltpu.VMEM((t + SUBLANES, LANES), jnp.float32),
                        pltpu.VMEM((t, LANES), jnp.float32),
                        pltpu.VMEM((t, LANES), jnp.float32)],
        compiler_params=_cparams("parallel", "parallel"),
        name="rglru",
    )(proj3, proj3, conv_state, h0, w_conv, b_conv, wa_bd, b_a, wx_bd, b_x, lam_c)


def _ordered_key(x):
    bits = lax.bitcast_convert_type(x, jnp.int32)
    return jnp.where(bits < 0, bits ^ jnp.int32(0x7FFFFFFF), bits)


def _kth_largest_key(key_ref, k):
    rows = key_ref.shape[0]

    def body(it, thr):
        cand = thr ^ lax.shift_left(jnp.int32(1), 31 - it)
        cnt = jnp.sum((key_ref[...] >= cand).astype(jnp.float32), axis=1, keepdims=True)
        return jnp.where(cnt >= k, cand, thr)

    return lax.fori_loop(0, 32, body, jnp.full((rows, 1), INT_MIN, jnp.int32))


def _selection_bias(key_ref, bias_ref, k, tri):
    rows, s_len = key_ref.shape
    thr = _kth_largest_key(key_ref, k)
    need = k - jnp.sum((key_ref[...] > thr).astype(jnp.float32), axis=1, keepdims=True)
    seen = jnp.zeros((rows, 1), jnp.float32)
    for c in range(s_len // LANES):
        sl = slice(c * LANES, (c + 1) * LANES)
        key = key_ref[:, sl]
        eq = key == thr
        rank = _dot(eq.astype(MXU_DTYPE), tri) + seen
        take = ((key > thr) | (eq & (rank <= need))) & (key > NEG_INF_KEY)
        bias_ref[:, sl] = jnp.where(take, 0.0, NEG_INF)
        seen = rank[:, LANES - 1:LANES]


def _dsa_prompt_kernel(qi_ref, ki_ref, tail_ref, q_ref, k_ref, v_ref, tri_ref, o_ref, key_s, bias_s, *, topk):
    tq = q_ref.shape[0]
    s_len = k_ref.shape[0]
    qpos = pl.program_id(1) * tq + lax.broadcasted_iota(jnp.int32, (tq, s_len), 0)
    kpos = lax.broadcasted_iota(jnp.int32, (tq, s_len), 1)
    causal = kpos <= qpos
    ki = ki_ref[...]
    sc = jnp.zeros((tq, s_len), jnp.float32)
    for h in range(N_IDX_HEADS):
        dots = _dot_nt(qi_ref[:, h * D_IDX:(h + 1) * D_IDX], ki)
        w = tail_ref[:, TAIL_WI + h:TAIL_WI + h + 1] * (N_IDX_HEADS ** -0.5)
        sc = sc + jnp.maximum(dots, 0.0) * w
    key_s[...] = _ordered_key(jnp.where(causal, sc, NEG_INF))
    _selection_bias(key_s, bias_s, topk, tri_ref[...])
    for h in range(HC):
        sl = slice(h * DHC, (h + 1) * DHC)
        logits = _dot_nt(q_ref[:, sl], k_ref[:, sl]) * (DHC ** -0.5) + bias_s[...]
        mx = jnp.max(logits, axis=1, keepdims=True)
        e = jnp.exp(logits - mx)
        p = e / jnp.sum(e, axis=1, keepdims=True)
        o_ref[:, sl] = _dot(p.astype(MXU_DTYPE), v_ref[:, sl]).astype(o_ref.dtype)


def _dsa_prompt(qi, ki, proj3, q, k, v, tri, topk, tq):
    bsz, s_len, _ = q.shape
    blk = lambda w: pl.BlockSpec((None, tq, w), lambda b, i: (b, i, 0))
    full = lambda w: pl.BlockSpec((None, s_len, w), lambda b, i: (b, 0, 0))
    return pl.pallas_call(
        functools.partial(_dsa_prompt_kernel, topk=topk),
        grid=(bsz, s_len // tq),
        in_specs=[blk(N_IDX_HEADS * D_IDX), full(D_IDX),
                  pl.BlockSpec((None, tq, LANES), lambda b, i: (b, i, OFF_TAIL // LANES)),
                  blk(D_C), full(D_C), full(D_C),
                  pl.BlockSpec((LANES, LANES), lambda b, i: (0, 0))],
        out_specs=blk(D_C),
        out_shape=jax.ShapeDtypeStruct((bsz, s_len, D_C), MXU_DTYPE),
        scratch_shapes=[pltpu.VMEM((tq, s_len), jnp.int32), pltpu.VMEM((tq, s_len), jnp.float32)],
        compiler_params=_cparams("parallel", "arbitrary"),
        name="dsa_prompt",
    )(qi, ki, proj3, q, k, v, tri)


def _sample_scores(qi, w, ki, ts):
    dots = jnp.maximum(_dot_nt(qi, ki), 0.0)
    sc = jnp.zeros((ts, ki.shape[0]), jnp.float32)
    for h in range(N_IDX_HEADS):
        wh = w[:, TAIL_WI + h:TAIL_WI + h + 1] * (N_IDX_HEADS ** -0.5)
        sc = sc + dots[h * ts:(h + 1) * ts, :] * wh
    return sc


def _dsa_sample_scores_kernel(pt_ref, qi_ref, w_ref, kin_ref, *rest, pages):
    ki_refs = rest[:pages]
    past_ref, new_ref = rest[pages:]
    ts = w_ref.shape[0]
    qi = qi_ref[...]
    w = w_ref[...]
    for r in range(pages):
        past_ref[:, r * PAGE_SIZE:(r + 1) * PAGE_SIZE] = _sample_scores(qi, w, ki_refs[r][...].astype(MXU_DTYPE), ts)

    @pl.when(pl.program_id(1) == 0)
    def _():
        sc = _sample_scores(qi, w, kin_ref[...], ts)
        t_i = lax.broadcasted_iota(jnp.int32, sc.shape, 0)
        s_i = lax.broadcasted_iota(jnp.int32, sc.shape, 1)
        new_ref[...] = jnp.where(s_i <= t_i, sc, NEG_INF)


def _dsa_sample_scores(page_table, layer, qi_s, proj3, kin, cache_ik, pages):
    bd, ts, _ = proj3.shape
    n_pages = page_table.shape[1]
    page_spec = lambda r: pl.BlockSpec((None, None, PAGE_SIZE, D_IDX),
                                       lambda b, j, pt: (layer, pt[b, j * pages + r], 0, 0))
    grid_spec = pltpu.PrefetchScalarGridSpec(
        num_scalar_prefetch=1,
        grid=(bd, n_pages // pages),
        in_specs=[pl.BlockSpec((None, N_IDX_HEADS * ts, D_IDX), lambda b, j, pt: (b, 0, 0)),
                  pl.BlockSpec((None, ts, LANES), lambda b, j, pt: (b, 0, OFF_TAIL // LANES)),
                  pl.BlockSpec((None, PAGE_SIZE, D_IDX), lambda b, j, pt: (b, 0, 0))]
                 + [page_spec(r) for r in range(pages)],
        out_specs=[pl.BlockSpec((None, ts, pages * PAGE_SIZE), lambda b, j, pt: (b, 0, j)),
                   pl.BlockSpec((None, ts, PAGE_SIZE), lambda b, j, pt: (b, 0, 0))],
    )
    return pl.pallas_call(
        functools.partial(_dsa_sample_scores_kernel, pages=pages),
        grid_spec=grid_spec,
        out_shape=[jax.ShapeDtypeStruct((bd, ts, n_pages * PAGE_SIZE), jnp.float32),
                   jax.ShapeDtypeStruct((bd, ts, PAGE_SIZE), jnp.float32)],
        compiler_params=_cparams("parallel", "arbitrary"),
        name="dsa_sample_scores",
    )(page_table, qi_s, proj3, kin, *([cache_ik] * pages))


def _dsa_sample_select_kernel(past_ref, new_ref, tri_ref, bias_ref, key_s, *, topk):
    n_past = past_ref.shape[1]
    key_s[:, 0:n_past] = _ordered_key(past_ref[...])
    key_s[:, n_past:] = _ordered_key(new_ref[...])
    _selection_bias(key_s, bias_ref, topk, tri_ref[...])


def _dsa_sample_select(sc_past, sc_new, tri, topk):
    rows, n_past = sc_past.shape
    n_all = n_past + sc_new.shape[1]
    return pl.pallas_call(
        functools.partial(_dsa_sample_select_kernel, topk=topk),
        out_shape=jax.ShapeDtypeStruct((rows, n_all), jnp.float32),
        scratch_shapes=[pltpu.VMEM((rows, n_all), jnp.int32)],
        compiler_params=pltpu.CompilerParams(vmem_limit_bytes=VMEM_LIMIT),
        name="dsa_sample_select",
    )(sc_past, sc_new, tri)


def _dsa_sample_attn_kernel(pt_ref, q_ref, bias_ref, biasn_ref, kn_ref, vn_ref, *rest, pages):
    k_refs = rest[:pages]
    v_refs = rest[pages:2 * pages]
    o_ref, m_s, l_s, acc_s = rest[2 * pages:]
    ts = q_ref.shape[0]
    j = pl.program_id(1)

    @pl.when(j == 0)
    def _():
        m_s[...] = jnp.full_like(m_s, -1e30)
        l_s[...] = jnp.zeros_like(l_s)
        acc_s[...] = jnp.zeros_like(acc_s)

    def update(k_of, v_of, bias):
        logits = jnp.concatenate(
            [_dot_nt(q_ref[:, h * DHC:(h + 1) * DHC], k_of(h)) * (DHC ** -0.5) + bias for h in range(HC)], axis=0)
        m_old = m_s[...]
        m_new = jnp.maximum(m_old, jnp.max(logits, axis=1, keepdims=True))
        alpha = jnp.exp(m_old - m_new)
        p = jnp.exp(logits - m_new)
        l_s[...] = alpha * l_s[...] + jnp.sum(p, axis=1, keepdims=True)
        pv = jnp.concatenate(
            [_dot(p[h * ts:(h + 1) * ts, :].astype(MXU_DTYPE), v_of(h)) for h in range(HC)], axis=0)
        acc_s[...] = alpha * acc_s[...] + pv
        m_s[...] = m_new

    for r in range(pages):
        update(lambda h, r=r: k_refs[r][:, h, :].astype(MXU_DTYPE),
               lambda h, r=r: v_refs[r][:, h, :].astype(MXU_DTYPE),
               bias_ref[:, r * PAGE_SIZE:(r + 1) * PAGE_SIZE])

    @pl.when(j == pl.num_programs(1) - 1)
    def _():
        update(lambda h: kn_ref[:, h * DHC:(h + 1) * DHC], lambda h: vn_ref[:, h * DHC:(h + 1) * DHC], biasn_ref[...])
        res = acc_s[...] / l_s[...]
        for h in range(HC):
            o_ref[:, h * DHC:(h + 1) * DHC] = res[h * ts:(h + 1) * ts, :].astype(o_ref.dtype)


def _dsa_sample_attn(page_table, layer, q, bias, k_new, v_new, cache_k, cache_v, pages):
    bd, ts, _ = q.shape
    n_pages = page_table.shape[1]
    page_spec = lambda r: pl.BlockSpec((None, None, PAGE_SIZE, HC, DHC),
                                       lambda b, j, pt: (layer, pt[b, j * pages + r], 0, 0, 0))
    new_spec = pl.BlockSpec((None, PAGE_SIZE, D_C), lambda b, j, pt: (b, 0, 0))
    grid_spec = pltpu.PrefetchScalarGridSpec(
        num_scalar_prefetch=1,
        grid=(bd, n_pages // pages),
        in_specs=[pl.BlockSpec((None, ts, D_C), lambda b, j, pt: (b, 0, 0)),
                  pl.BlockSpec((None, ts, pages * PAGE_SIZE), lambda b, j, pt: (b, 0, j)),
                  pl.BlockSpec((None, ts, PAGE_SIZE), lambda b, j, pt: (b, 0, n_pages)),
                  new_spec, new_spec]
                 + [page_spec(r) for r in range(pages)] * 2,
        out_specs=pl.BlockSpec((None, ts, D_C), lambda b, j, pt: (b, 0, 0)),
        scratch_shapes=[pltpu.VMEM((HC * ts, LANES), jnp.float32)] * 3,
    )
    return pl.pallas_call(
        functools.partial(_dsa_sample_attn_kernel, pages=pages),
        grid_spec=grid_spec,
        out_shape=jax.ShapeDtypeStruct((bd, ts, D_C), MXU_DTYPE),
        compiler_params=_cparams("parallel", "arbitrary"),
        name="dsa_sample_attn",
    )(page_table, q, bias, bias, k_new, v_new, *([cache_k] * pages), *([cache_v] * pages))


class _Tiles:
    def __init__(self, **kw):
        self.__dict__.update(kw)


def _prompt_tiles(rows, t):
    return _Tiles(proj_m=min(1024, rows), proj_n=768, out_m=min(256, rows), ffn_m=min(512, rows), ffn_f=512,
                  rope_m=min(256, rows), chunk=min(128, t), mlstm_t=min(512, t), rglru_rows=min(256, t))


def _sample_tiles(rows, ts):
    return _Tiles(proj_m=rows, proj_n=768, out_m=rows, ffn_m=rows, ffn_f=512, rope_m=rows,
                  chunk=ts, mlstm_t=ts, rglru_rows=ts)


def _layer_front(xb, bsz, t, wl, tabs, mstate, conv_state, h0, tiles):
    proj = _proj(xb, wl["w_in"], wl["b_in"], tiles.proj_m, tiles.proj_n)
    proj3 = proj.reshape(bsz, t, N_PROJ)
    gcol = proj3[:, :, OFF_TAIL + TAIL_IA:OFF_TAIL + TAIL_IA + 2 * HA]
    grow = gcol.reshape(bsz, t // tiles.chunk, tiles.chunk, 2 * HA).transpose(0, 1, 3, 2)
    c0, n0, m0 = mstate
    m0b = jnp.broadcast_to(m0[:, :, None], (bsz, HA, DHA))
    h_a, c1, n1, m1 = _mlstm(proj3, gcol, grow, c0, n0, m0b, wl["gn"], tiles.chunk, tiles.mlstm_t)
    y_b, new_buf, h1 = _rglru(proj3, conv_state, h0[:, None, :], wl["w_conv"], wl["b_conv"], wl["wa_bd"], wl["b_a"],
                              wl["wx_bd"], wl["b_x"], wl["lam_c"], tiles.rglru_rows)
    rope_out = _rope(proj, tabs, tiles.rope_m)
    return proj3, h_a, y_b, rope_out, (c1, n1, m1[:, :, 0]), new_buf, h1[:, 0, :]


def _layer_back(h_a, y_b, y_c, xf, xb, wl, tiles):
    m = xf.shape[0]
    xf, xb = _outproj(h_a.reshape(m, D_A), y_b.reshape(m, D_B), y_c.reshape(m, D_C), wl["w_out"], xf,
                      wl["ln1_g"], wl["ln1_b"], tiles.out_m)
    return _ffn(xb, xf, wl["w_gu"], wl["w_down"], wl["ln2_g"], wl["ln2_b"], tiles.ffn_m, tiles.ffn_f)


def _prep_weights(l, w_in, b_in, g_mlstm, w_conv, b_conv, w_a, b_a, w_x, b_x, lam, w_out, ln1_g, ln1_b, w_gu, w_down,
                  ln2_g, ln2_b):
    sec = np.cumsum([0, D_A, D_A, D_A, D_A, HA, HA, D_B, D_B, D_C, D_C, D_C, N_IDX_HEADS * D_IDX, D_IDX, N_IDX_HEADS])
    order = [(sec[0], sec[4]), (sec[6], sec[8]), (sec[8], sec[11]), (sec[11], sec[12]), (sec[12], sec[13]),
             (sec[13], sec[14]), (sec[4], sec[6])]

    def permute(a):
        parts = [a[..., lo:hi] for lo, hi in order]
        used = sum(hi - lo for lo, hi in order)
        parts.append(jnp.zeros(a.shape[:-1] + (N_PROJ - used,), a.dtype))
        return jnp.concatenate(parts, axis=-1)

    def block_diag_pairs(w):
        per = LANES // BLK_B
        wt = w.reshape(D_B // LANES, per, BLK_B, BLK_B)
        eye = jnp.eye(per, dtype=w.dtype)
        return jnp.einsum("cpij,pq->cpiqj", wt, eye).reshape(D_B // LANES, LANES, LANES).astype(MXU_DTYPE)

    row = lambda a: a[l][None, :]
    return dict(
        w_in=permute(w_in[l]).astype(MXU_DTYPE), b_in=permute(b_in[l])[None, :],
        gn=g_mlstm[l].reshape(HA, DHA),
        w_conv=w_conv[l], b_conv=row(b_conv), wa_bd=block_diag_pairs(w_a[l]), b_a=row(b_a),
        wx_bd=block_diag_pairs(w_x[l]), b_x=row(b_x),
        lam_c=(-LRU_C * jax.nn.softplus(-lam[l].astype(jnp.float32)))[None, :],
        w_out=w_out[l].astype(MXU_DTYPE), ln1_g=row(ln1_g), ln1_b=row(ln1_b),
        w_gu=w_gu[l].astype(MXU_DTYPE), w_down=w_down[l].astype(MXU_DTYPE), ln2_g=row(ln2_g), ln2_b=row(ln2_b))


def kernel(x_prompt, x_sample, cache_k, cache_v, cache_idx_k, page_table, state_mlstm_c, state_mlstm_n, state_mlstm_m, state_conv, state_rglru_h, w_in, b_in, g_mlstm, w_conv, b_conv, w_a, b_a, w_x, b_x, lam, w_out, ln1_g, ln1_b, w_gu, w_down, ln2_g, ln2_b):
    bp, tp, _ = x_prompt.shape
    bs, ts, _ = x_sample.shape
    n_pages = page_table.shape[1]
    past = n_pages * PAGE_SIZE
    f32 = jnp.float32
    sdt = state_mlstm_c.dtype

    tri = (np.arange(LANES)[:, None] <= np.arange(LANES)[None, :]).astype(np.float32)
    tri = jnp.asarray(tri, MXU_DTYPE)
    tabs_p = _rope_tables(jnp.arange(tp), bp)
    tabs_s = _rope_tables(past + jnp.arange(ts), bs)
    tiles_p = _prompt_tiles(bp * tp, tp)
    tiles_s = _sample_tiles(bs * ts, ts)
    topk_p = min(TOPK_MAX, tp // 4)
    topk_s = min(TOPK_MAX, (past + ts) // 4)
    score_pages = min(16, n_pages)
    attn_pages = min(4, n_pages)

    xpf = x_prompt.reshape(bp * tp, D_MODEL)
    xsf = x_sample.reshape(bs * ts, D_MODEL)
    xpb = xpf.astype(MXU_DTYPE)
    xsb = xsf.astype(MXU_DTYPE)
    zero_m = (jnp.zeros((bp, HA, DHA, DHA), sdt), jnp.zeros((bp, HA, DHA), sdt), jnp.zeros((bp, HA), sdt))
    zero_conv = jnp.zeros((bp, CONV_W - 1, D_B), x_prompt.dtype)
    zero_h = jnp.zeros((bp, D_B), state_rglru_h.dtype)

    acc = [[] for _ in range(16)]
    for l in range(DEPTH):
        wl = _prep_weights(l, w_in, b_in, g_mlstm, w_conv, b_conv, w_a, b_a, w_x, b_x, lam, w_out, ln1_g, ln1_b,
                           w_gu, w_down, ln2_g, ln2_b)
        proj3, h_a, y_b, rope_out, m_p, buf_p, h_p = _layer_front(
            xpb, bp, tp, wl, tabs_p, zero_m, zero_conv, zero_h, tiles_p)
        kf, vf, kif, qb, kb, vb, qib, kib = rope_out
        g3 = lambda a: a.reshape(bp, tp, a.shape[-1])
        y_c = _dsa_prompt(g3(qib), g3(kib), proj3, g3(qb), g3(kb), g3(vb), tri, topk_p, min(128, tp))
        xpf, xpb = _layer_back(h_a, y_b, y_c, xpf, xpb, wl, tiles_p)
        kv_p = (kf.reshape(bp, tp, HC, DHC), vf.reshape(bp, tp, HC, DHC), kif.reshape(bp, tp, D_IDX))

        proj3, h_a, y_b, rope_out, m_s, buf_s, h_s = _layer_front(
            xsb, bs, ts, wl, tabs_s, (state_mlstm_c[l], state_mlstm_n[l], state_mlstm_m[l]), state_conv[l],
            state_rglru_h[l], tiles_s)
        kf, vf, kif, qb, kb, vb, qib, kib = rope_out
        qi_s = qib.reshape(bs, ts, N_IDX_HEADS, D_IDX).transpose(0, 2, 1, 3).reshape(bs, N_IDX_HEADS * ts, D_IDX)
        pad_page = lambda a: jnp.pad(a.reshape(bs, ts, a.shape[-1]), ((0, 0), (0, PAGE_SIZE - ts), (0, 0)))
        sc_past, sc_new = _dsa_sample_scores(page_table, l, qi_s, proj3, pad_page(kib), cache_idx_k, score_pages)
        bias = _dsa_sample_select(sc_past.reshape(bs * ts, past), sc_new.reshape(bs * ts, PAGE_SIZE), tri, topk_s)
        y_c = _dsa_sample_attn(page_table, l, qb.reshape(bs, ts, D_C), bias.reshape(bs, ts, past + PAGE_SIZE),
                               pad_page(kb), pad_page(vb), cache_k, cache_v, attn_pages)
        xsf, xsb = _layer_back(h_a, y_b, y_c, xsf, xsb, wl, tiles_s)
        kv_s = (kf.reshape(bs, ts, HC, DHC), vf.reshape(bs, ts, HC, DHC), kif.reshape(bs, ts, D_IDX))

        for j, a in enumerate(kv_p + kv_s + m_p + (buf_p, h_p) + m_s + (buf_s, h_s)):
            acc[j].append(a)

    outs = [jnp.stack(a) for a in acc]
    return (xpf.reshape(bp, tp, D_MODEL), xsf.reshape(bs, ts, D_MODEL), *outs)
```
